```python
import jax
import jax.numpy as jnp
from jax import lax
import numpy as np

D_MODEL = 2048
BATCH = 1
SEQ = 16384
DEPTH = 2
DEC_BATCH = 8
DEC_SEQ = 64
PAST_LEN = 4096

CHUNK = 64
HEAD_DIM = 128
SCALE = HEAD_DIM ** -0.5
NORM_EPS = 1e-6
N_A = 4
A_BAND = 8
A_REACH = A_BAND * CHUNK
REL_CLIP = 256
N_B = 6
Q_BLOCK = 128
N_C = 12
C_HEAD = 64
LNX_EPS = 64e-5
W_A = N_A * HEAD_DIM
W_B = N_B * HEAD_DIM
W_C = N_C * C_HEAD
LORA_W = 96
LORA_A = 96
LORA_G = 256
C_PROJ = 3 * W_C + LORA_W + LORA_A + LORA_G
C_SPLITS = (W_C, 2 * W_C, 3 * W_C, 3 * W_C + LORA_W, 3 * W_C + LORA_W + LORA_A)
N_MEM = 256
N_X = 4
W_X = N_X * HEAD_DIM
D_FF = -(-8 * D_MODEL // (3 * 256)) * 256
OFF_B = 3 * W_A
OFF_F = OFF_B + 3 * W_B
OFF_C = OFF_F + N_B
OFF_G = OFF_C + C_PROJ
IN_COLS = OFF_G + 3 * D_MODEL

kernel_name = 'hybrid_streaming_encoder_step'


def rms_norm(x, g):
    xf = x.astype(jnp.float32)
    y = xf * lax.rsqrt(jnp.mean(xf * xf, axis=-1, keepdims=True) + NORM_EPS)
    return (y * g.astype(jnp.float32)).astype(x.dtype)


def to_heads(t, n, d):
    return t.reshape(t.shape[:-1] + (n, d))


def rel_bias_lookup(table, dist):
    return table[:, jnp.clip(dist, -REL_CLIP, REL_CLIP) + REL_CLIP].astype(jnp.float32)


def chunk_attn_prompt(q, k, v, table):
    B, T, H, Dh = q.shape
    nc = T // CHUNK
    band = (A_BAND + 1) * CHUNK
    pad = jnp.zeros((B, A_REACH, H, Dh), k.dtype)
    kp = jnp.concatenate([pad, k], axis=1).reshape(B, nc + A_BAND, CHUNK, H, Dh)
    vp = jnp.concatenate([pad.astype(v.dtype), v], axis=1).reshape(B, nc + A_BAND, CHUNK, H, Dh)
    kb = jnp.concatenate([kp[:, j:j + nc] for j in range(A_BAND + 1)], axis=2)
    vb = jnp.concatenate([vp[:, j:j + nc] for j in range(A_BAND + 1)], axis=2)
    qc = q.reshape(B, nc, CHUNK, H, Dh)
    s = jnp.einsum('bcqhd,bckhd->bchqk', qc, kb, preferred_element_type=jnp.float32) * SCALE
    qi = jnp.arange(CHUNK)[:, None]
    kj = jnp.arange(band)[None, :]
    s = s + rel_bias_lookup(table, A_REACH + qi - kj)
    valid = jnp.arange(nc)[:, None] * CHUNK + kj >= A_REACH
    s = jnp.where(valid[None, :, None, None, :], s, -jnp.inf)
    p = jax.nn.softmax(s, axis=-1).astype(v.dtype)
    o = jnp.einsum('bchqk,bckhd->bcqhd', p, vb)
    return o.reshape(B, T, H * Dh)


def chunk_attn_sample(q, k, v, k_cache, v_cache, table):
    B, n, H, Dh = q.shape
    L = k_cache.shape[1]
    kk = jnp.concatenate([k_cache.astype(k.dtype), k], axis=1)
    vv = jnp.concatenate([v_cache.astype(v.dtype), v], axis=1)
    s = jnp.einsum('bqhd,bkhd->bhqk', q, kk, preferred_element_type=jnp.float32) * SCALE
    dist = L + jnp.arange(n)[:, None] - jnp.arange(L + n)[None, :]
    s = s + rel_bias_lookup(table, dist)[None]
    p = jax.nn.softmax(s, axis=-1).astype(vv.dtype)
    return jnp.einsum('bhqk,bkhd->bqhd', p, vv).reshape(B, n, H * Dh)


def fox_prompt(q, k, v, logf):
    B, T, H, Dh = q.shape
    nb = T // Q_BLOCK
    c = jnp.cumsum(logf, axis=1).transpose(0, 2, 1)
    qb = q.reshape(B, nb, Q_BLOCK, H, Dh).transpose(1, 0, 2, 3, 4)
    cb = c.reshape(B, H, nb, Q_BLOCK).transpose(2, 0, 1, 3)
    kpos = jnp.arange(T)

    def block(args):
        i, q_i, c_i = args
        s = jnp.einsum('bqhd,bkhd->bhqk', q_i, k, preferred_element_type=jnp.float32) * SCALE
        s = s + c_i[..., :, None] - c[..., None, :]
        qpos = i * Q_BLOCK + jnp.arange(Q_BLOCK)
        s = jnp.where(kpos[None, :] <= qpos[:, None], s, -jnp.inf)
        p = jax.nn.softmax(s, axis=-1).astype(v.dtype)
        return jnp.einsum('bhqk,bkhd->bqhd', p, v)

    o = lax.map(block, (jnp.arange(nb), qb, cb))
    return o.transpose(1, 0, 2, 3, 4).reshape(B, T, H * Dh)


def fox_sample(q, k, v, logf, k_cache, v_cache, logf_cache):
    B, n, H, Dh = q.shape
    P = k_cache.shape[1]
    kk = jnp.concatenate([k_cache.astype(k.dtype), k], axis=1)
    vv = jnp.concatenate([v_cache.astype(v.dtype), v], axis=1)
    c = jnp.cumsum(jnp.concatenate([logf_cache.astype(jnp.float32), logf], axis=1), axis=1).transpose(0, 2, 1)
    s = jnp.einsum('bqhd,bkhd->bhqk', q, kk, preferred_element_type=jnp.float32) * SCALE
    s = s + c[..., P:, None] - c[..., None, :]
    causal = jnp.arange(P + n)[None, :] <= P + jnp.arange(n)[:, None]
    s = jnp.where(causal, s, -jnp.inf)
    p = jax.nn.softmax(s, axis=-1).astype(vv.dtype)
    return jnp.einsum('bhqk,bkhd->bqhd', p, vv).reshape(B, n, H * Dh)


def wkv_scan(r, w, k, v, a, b, s0):
    def step(s, inp):
        r_t, w_t, k_t, v_t, a_t, b_t = inp
        sa = jnp.einsum('bhvk,bhk->bhv', s, a_t)
        s = s * w_t[:, :, None, :] + sa[..., None] * b_t[:, :, None, :] + v_t[..., None] * k_t[:, :, None, :]
        return s, jnp.einsum('bhvk,bhk->bhv', s, r_t)

    xs = tuple(t.swapaxes(0, 1) for t in (r, w, k, v, a, b))
    s, y = lax.scan(step, s0, xs)
    return y.swapaxes(0, 1), s


def rwkv_branch(pc, shift_prev, wkv0, lp):
    B, T, _ = pc.shape
    f32 = jnp.float32
    prev = jnp.concatenate([shift_prev.astype(pc.dtype), pc[:, :-1]], axis=1)
    z = pc + (prev - pc) * lp['c_mu']
    r, k, v, zw, za, zg = jnp.split(z, C_SPLITS, axis=-1)
    w_raw = -jax.nn.softplus(-(lp['c_w0'] + jnp.tanh(zw) @ lp['c_w2']).astype(f32)) - 0.5
    decay = jnp.exp(-jnp.exp(w_raw))
    a = jax.nn.sigmoid((lp['c_a0'] + za @ lp['c_a2']).astype(f32))
    gate = (jax.nn.sigmoid(zg) @ lp['c_g2']).astype(f32)
    r, k, v, a, decay = (to_heads(t.astype(f32), N_C, C_HEAD) for t in (r, k, v, a, decay))
    kk = k * to_heads(lp['c_k_k'].astype(f32), N_C, C_HEAD)
    kk = kk * lax.rsqrt(jnp.sum(kk * kk, axis=-1, keepdims=True) + 1e-12)
    k = k * (1.0 + (a - 1.0) * to_heads(lp['c_k_a'].astype(f32), N_C, C_HEAD))
    y, s = wkv_scan(r, decay, k, v, -kk, kk * a, wkv0.astype(f32))
    mu = jnp.mean(y, axis=-1, keepdims=True)
    var = jnp.mean(jnp.square(y - mu), axis=-1, keepdims=True)
    yn = ((y - mu) * lax.rsqrt(var + LNX_EPS)).reshape(B, T, W_C) * lp['c_lnx_w'] + lp['c_lnx_b']
    bonus = jnp.sum(r * k * lp['c_r_k'].astype(f32), axis=-1, keepdims=True) * v
    out = ((yn + bonus.reshape(B, T, W_C)) * gate).astype(pc.dtype)
    return out, pc[:, -1:], s


def memory_kv(mem, g_mem, w_mkv):
    mk, mv = jnp.split(rms_norm(mem, g_mem) @ w_mkv, 2, axis=-1)
    return to_heads(mk, N_X, HEAD_DIM), to_heads(mv, N_X, HEAD_DIM)


def cross_attn(h, mk, mv, w_xq, w_xo):
    B, T, _ = h.shape
    q = to_heads(h @ w_xq, N_X, HEAD_DIM)
    s = jnp.einsum('bqhd,bkhd->bhqk', q, mk.astype(q.dtype), preferred_element_type=jnp.float32) * SCALE
    p = jax.nn.softmax(s, axis=-1).astype(q.dtype)
    o = jnp.einsum('bhqk,bkhd->bqhd', p, mv.astype(q.dtype))
    return o.reshape(B, T, W_X) @ w_xo


def swiglu(h, w_ffn_in, w_ffn_out):
    gte, up = jnp.split(h @ w_ffn_in, 2, axis=-1)
    return (jax.nn.silu(gte) * up) @ w_ffn_out


def layer(x, lp, mem=None, cache=None):
    B, T, _ = x.shape
    g = lp['norm_g']
    h = rms_norm(x, g[0])
    p = h @ lp['w_in']
    pa, pb, pf, pc, pg = jnp.split(p, (OFF_B, OFF_F, OFF_C, OFF_G), axis=-1)
    qa, ka, va = (to_heads(t, N_A, HEAD_DIM) for t in jnp.split(pa, 3, axis=-1))
    qf, kf, vf = (to_heads(t, N_B, HEAD_DIM) for t in jnp.split(pb, 3, axis=-1))
    logf = jax.nn.log_sigmoid(pf.astype(jnp.float32) + lp['b_forget_bias'].astype(jnp.float32))
    if cache is None:
        oa = chunk_attn_prompt(qa, ka, va, lp['a_rel_bias'])
        ob = fox_prompt(qf, kf, vf, logf)
        shift0 = jnp.zeros((B, 1, C_PROJ), pc.dtype)
        wkv0 = jnp.zeros((B, N_C, C_HEAD, C_HEAD), jnp.float32)
        mk, mv = memory_kv(mem, lp['mem_norm_g'], lp['w_mkv'])
    else:
        ak, av, bk, bv, blf, shift0, wkv0, mk, mv = cache
        oa = chunk_attn_sample(qa, ka, va, ak, av, lp['a_rel_bias'])
        ob = fox_sample(qf, kf, vf, logf, bk, bv, blf)
    oc, shift_new, wkv_new = rwkv_branch(pc, shift0, wkv0, lp)
    ga, gb, gc = jnp.split(jax.nn.sigmoid(pg), 3, axis=-1)
    mix = (ga * (oa @ lp['w_branch_a']) + gb * (ob @ lp['w_branch_b']) + gc * (oc @ lp['w_branch_c'])) @ lp['w_out']
    x = x + rms_norm(mix, g[1])
    x = x + rms_norm(cross_attn(rms_norm(x, g[2]), mk, mv, lp['w_xq'], lp['w_xo']), g[3])
    x = x + rms_norm(swiglu(rms_norm(x, g[4]), lp['w_ffn_in'], lp['w_ffn_out']), g[5])
    n_keep = min(A_REACH, T)
    rows = (ka[:, T - n_keep:], va[:, T - n_keep:], kf, vf, logf, shift_new, wkv_new)
    return x, rows, (mk, mv)


def setup_inputs(seed: int = 0) -> dict:
    key = jax.random.key(seed)
    ks = iter(jax.random.split(key, 64))
    f32 = jnp.float32

    def nrm(shape, scale=1.0):
        return jax.random.normal(next(ks), shape, f32) * scale

    def unif(shape, lo, hi):
        return jax.random.uniform(next(ks), shape, f32, lo, hi)

    D = D_MODEL
    LA = min(A_REACH, PAST_LEN)
    return {
        'x_prompt': nrm((BATCH, SEQ, D)),
        'x_sample': nrm((DEC_BATCH, DEC_SEQ, D)),
        'mem_prompt': nrm((BATCH, N_MEM, D)),
        'cache_a_k': nrm((DEPTH, DEC_BATCH, LA, N_A, HEAD_DIM)),
        'cache_a_v': nrm((DEPTH, DEC_BATCH, LA, N_A, HEAD_DIM)),
        'cache_b_k': nrm((DEPTH, DEC_BATCH, PAST_LEN, N_B, HEAD_DIM)),
        'cache_b_v': nrm((DEPTH, DEC_BATCH, PAST_LEN, N_B, HEAD_DIM)),
        'cache_b_logf': jax.nn.log_sigmoid(2.0 + nrm((DEPTH, DEC_BATCH, PAST_LEN, N_B), 0.5)),
        'state_c_shift': nrm((DEPTH, DEC_BATCH, 1, C_PROJ)),
        'state_c_wkv': nrm((DEPTH, DEC_BATCH, N_C, C_HEAD, C_HEAD), 0.3),
        'cache_m_k': nrm((DEPTH, DEC_BATCH, N_MEM, N_X, HEAD_DIM)),
        'cache_m_v': nrm((DEPTH, DEC_BATCH, N_MEM, N_X, HEAD_DIM)),
        'norm_g': 1.0 + nrm((DEPTH, 6, D), 0.05),
        'mem_norm_g': 1.0 + nrm((DEPTH, D), 0.05),
        'w_in': nrm((DEPTH, D, IN_COLS), D ** -0.5),
        'a_rel_bias': nrm((DEPTH, N_A, 2 * REL_CLIP + 1), 0.5),
        'b_forget_bias': 2.0 + nrm((DEPTH, N_B), 0.5),
        'c_mu': unif((DEPTH, C_PROJ), 0.0, 1.0),
        'c_w0': unif((DEPTH, W_C), -4.0, 1.0),
        'c_w2': nrm((DEPTH, LORA_W, W_C), 0.1 * LORA_W ** -0.5),
        'c_a0': nrm((DEPTH, W_C), 0.5),
        'c_a2': nrm((DEPTH, LORA_A, W_C), 0.5 * LORA_A ** -0.5),
        'c_g2': nrm((DEPTH, LORA_G, W_C), LORA_G ** -0.5),
        'c_k_k': 0.85 + nrm((DEPTH, W_C), 0.05),
        'c_k_a': 1.0 + nrm((DEPTH, W_C), 0.05),
        'c_r_k': nrm((DEPTH, N_C, C_HEAD), 0.1),
        'c_lnx_w': 1.0 + nrm((DEPTH, W_C), 0.05),
        'c_lnx_b': nrm((DEPTH, W_C), 0.02),
        'w_branch_a': nrm((DEPTH, W_A, D), W_A ** -0.5),
        'w_branch_b': nrm((DEPTH, W_B, D), W_B ** -0.5),
        'w_branch_c': nrm((DEPTH, W_C, D), W_C ** -0.5),
        'w_out': nrm((DEPTH, D, D), D ** -0.5),
        'w_xq': nrm((DEPTH, D, W_X), D ** -0.5),
        'w_mkv': nrm((DEPTH, D, 2 * W_X), D ** -0.5),
        'w_xo': nrm((DEPTH, W_X, D), W_X ** -0.5),
        'w_ffn_in': nrm((DEPTH, D, 2 * D_FF), D ** -0.5),
        'w_ffn_out': nrm((DEPTH, D_FF, D), D_FF ** -0.5),
    }


def reference(x_prompt, x_sample, mem_prompt, cache_a_k, cache_a_v, cache_b_k, cache_b_v, cache_b_logf,
              state_c_shift, state_c_wkv, cache_m_k, cache_m_v, norm_g, mem_norm_g, w_in, a_rel_bias,
              b_forget_bias, c_mu, c_w0, c_w2, c_a0, c_a2, c_g2, c_k_k, c_k_a, c_r_k, c_lnx_w, c_lnx_b,
              w_branch_a, w_branch_b, w_branch_c, w_out, w_xq, w_mkv, w_xo, w_ffn_in, w_ffn_out):
    y_p = x_prompt
    y_s = x_sample
    rows_p, mems_p, rows_s = [], [], []
    for l in range(DEPTH):
        lp = dict(norm_g=norm_g[l], mem_norm_g=mem_norm_g[l], w_in=w_in[l], a_rel_bias=a_rel_bias[l],
                  b_forget_bias=b_forget_bias[l], c_mu=c_mu[l], c_w0=c_w0[l], c_w2=c_w2[l], c_a0=c_a0[l],
                  c_a2=c_a2[l], c_g2=c_g2[l], c_k_k=c_k_k[l], c_k_a=c_k_a[l], c_r_k=c_r_k[l],
                  c_lnx_w=c_lnx_w[l], c_lnx_b=c_lnx_b[l], w_branch_a=w_branch_a[l], w_branch_b=w_branch_b[l],
                  w_branch_c=w_branch_c[l], w_out=w_out[l], w_xq=w_xq[l], w_mkv=w_mkv[l], w_xo=w_xo[l],
                  w_ffn_in=w_ffn_in[l], w_ffn_out=w_ffn_out[l])
        y_p, r_p, m_p = layer(y_p, lp, mem=mem_prompt)
        rows_p.append(r_p)
        mems_p.append(m_p)
        cache = (cache_a_k[l], cache_a_v[l], cache_b_k[l], cache_b_v[l], cache_b_logf[l],
                 state_c_shift[l], state_c_wkv[l], cache_m_k[l], cache_m_v[l])
        y_s, r_s, _ = layer(y_s, lp, cache=cache)
        rows_s.append(r_s)
    a_k_p, a_v_p, b_k_p, b_v_p, b_lf_p, c_sh_p, c_wkv_p = [jnp.stack(t) for t in zip(*rows_p)]
    m_k_p, m_v_p = [jnp.stack(t) for t in zip(*mems_p)]
    a_k_s, a_v_s, b_k_s, b_v_s, b_lf_s, c_sh_s, c_wkv_s = [jnp.stack(t) for t in zip(*rows_s)]
    return (y_p, y_s, a_k_p, a_v_p, b_k_p, b_v_p, b_lf_p, c_sh_p, c_wkv_p, m_k_p, m_v_p,
            a_k_s, a_v_s, b_k_s, b_v_s, b_lf_s, c_sh_s, c_wkv_s)
```

```python
import functools

import jax
import jax.numpy as jnp
from jax import lax
from jax.experimental import pallas as pl
from jax.experimental.pallas import tpu as pltpu

F32 = jnp.float32
BF16 = jnp.bfloat16
MXU_DTYPE = jnp.bfloat16

LANE = 128
VMEM_LIMIT = 56 * 1024 * 1024

NORM_EPS = 1e-6
LNX_EPS = 64e-5
HEAD_DIM = 128
SCALE = HEAD_DIM ** -0.5
CHUNK = 64
A_BAND = 8
A_REACH = A_BAND * CHUNK
REL_CLIP = 256
N_A = 4
N_B = 6
N_C = 12
C_HEAD = 64
W_C = N_C * C_HEAD
LORA_PAD = 128
N_X = 4
A_QBLK = 4 * CHUNK
A_KWIN = A_QBLK + A_REACH


def _params(*sem):
    return pltpu.CompilerParams(dimension_semantics=sem, vmem_limit_bytes=VMEM_LIMIT)


def _dot(a, b):
    return jnp.dot(a.astype(MXU_DTYPE), b.astype(MXU_DTYPE), preferred_element_type=F32)


def _dot_nt(a, b):
    return lax.dot_general(a.astype(MXU_DTYPE), b.astype(MXU_DTYPE), (((1,), (1,)), ((), ())),
                           preferred_element_type=F32)


def _dot_split(x, m):
    x1 = x.astype(BF16)
    r1 = x - x1.astype(F32)
    x2 = r1.astype(BF16)
    x3 = (r1 - x2.astype(F32)).astype(BF16)
    d = functools.partial(jnp.dot, preferred_element_type=F32)
    return d(x1, m) + d(x2, m) + d(x3, m)


def _rms(x, g):
    return x * lax.rsqrt(jnp.mean(x * x, axis=-1, keepdims=True) + NORM_EPS) * g


def _sigmoid(x):
    return 1.0 / (1.0 + jnp.exp(-x))


def _softplus(x):
    return jnp.maximum(x, 0.0) + jnp.log1p(jnp.exp(-jnp.abs(x)))


def _row_tile(m, cap):
    t = min(m, cap)
    assert m % t == 0, (m, t)
    return t


def _rms_matmul_kernel(x_ref, g_ref, w_ref, o_ref, h_ref):
    @pl.when(pl.program_id(1) == 0)
    def _():
        h_ref[...] = _rms(x_ref[...], g_ref[...]).astype(h_ref.dtype)

    o_ref[...] = jnp.dot(h_ref[...], w_ref[...], preferred_element_type=F32)


def _rms_matmul(x, g, w, tn):
    m, d = x.shape
    n = w.shape[1]
    tm = _row_tile(m, 512)
    assert n % tn == 0
    return pl.pallas_call(
        _rms_matmul_kernel,
        out_shape=jax.ShapeDtypeStruct((m, n), F32),
        grid=(m // tm, n // tn),
        in_specs=[pl.BlockSpec((tm, d), lambda i, j: (i, 0)),
                  pl.BlockSpec((1, d), lambda i, j: (0, 0)),
                  pl.BlockSpec((d, tn), lambda i, j: (0, j))],
        out_specs=pl.BlockSpec((tm, tn), lambda i, j: (i, j)),
        scratch_shapes=[pltpu.VMEM((tm, d), MXU_DTYPE)],
        compiler_params=_params("parallel", "arbitrary"),
        name="rms_matmul",
    )(x, g.reshape(1, d), w)


def _matmul_norm_res_kernel(a_ref, w_ref, g_ref, x_ref, o_ref):
    y = jnp.dot(a_ref[...], w_ref[...], preferred_element_type=F32)
    o_ref[...] = x_ref[...] + _rms(y, g_ref[...])


def _matmul_norm_res(a, w, g, x):
    m, k = a.shape
    d = w.shape[1]
    tm = _row_tile(m, 512)
    return pl.pallas_call(
        _matmul_norm_res_kernel,
        out_shape=jax.ShapeDtypeStruct((m, d), F32),
        grid=(m // tm,),
        in_specs=[pl.BlockSpec((tm, k), lambda i: (i, 0)),
                  pl.BlockSpec((k, d), lambda i: (0, 0)),
                  pl.BlockSpec((1, d), lambda i: (0, 0)),
                  pl.BlockSpec((tm, d), lambda i: (i, 0))],
        out_specs=pl.BlockSpec((tm, d), lambda i: (i, 0)),
        compiler_params=_params("parallel"),
        name="matmul_norm_res",
    )(a, w, g.reshape(1, d), x)


def _rel_bias_window(frow, n_rows):
    width = frow.shape[1]
    x = jnp.broadcast_to(frow, (n_rows, width))
    row = lax.broadcasted_iota(jnp.int32, (n_rows, width), 0)
    step = 1
    while step < n_rows:
        x = jnp.where((row & step) != 0, pltpu.roll(x, step, axis=1), x)
        step *= 2
    return x


def _chunk_attn_prompt_kernel(q_ref, k0_ref, k1_ref, k2_ref, v0_ref, v1_ref, v2_ref, f_ref, o_ref, bias_ref):
    i = pl.program_id(1)

    @pl.when(i == 0)
    def _():
        skew = _rel_bias_window(f_ref[...].reshape(1, -1), A_QBLK)
        b = pltpu.roll(skew, 2 * A_KWIN - A_QBLK, axis=1)[:, :A_KWIN]
        qc = lax.broadcasted_iota(jnp.int32, (A_QBLK, A_KWIN), 0) // CHUNK
        kc = lax.broadcasted_iota(jnp.int32, (A_QBLK, A_KWIN), 1) // CHUNK
        bias_ref[...] = jnp.where((kc >= qc) & (kc <= qc + A_BAND), b, -jnp.inf)

    q = q_ref[...]
    k = jnp.concatenate([k0_ref[...], k1_ref[...], k2_ref[...]], axis=0)
    v = jnp.concatenate([v0_ref[...], v1_ref[...], v2_ref[...]], axis=0)
    s = _dot_nt(q, k) * SCALE + bias_ref[...]
    kpos = i * A_QBLK - A_REACH + lax.broadcasted_iota(jnp.int32, s.shape, 1)
    s = jnp.where(kpos >= 0, s, -jnp.inf)
    m = jnp.max(s, axis=-1, keepdims=True)
    e = jnp.exp(s - m)
    p = e / jnp.sum(e, axis=-1, keepdims=True)
    o_ref[...] = _dot(p, v).astype(o_ref.dtype)


def _rel_bias_rows(table):
    m = jnp.arange(2 * A_KWIN)
    idx = jnp.clip(A_KWIN - m, -REL_CLIP, REL_CLIP) + REL_CLIP
    return table[:, idx].reshape(N_A, 1, 2 * A_KWIN)


def _chunk_attn_prompt(p_ab, frow):
    t = p_ab.shape[0]
    assert t % A_QBLK == 0
    nq = t // A_QBLK
    qspec = pl.BlockSpec((A_QBLK, HEAD_DIM), lambda h, i: (i, h))

    def kv(off, back):
        return pl.BlockSpec((A_QBLK, HEAD_DIM), lambda h, i: (jnp.maximum(i - back, 0), off + h))

    return pl.pallas_call(
        _chunk_attn_prompt_kernel,
        out_shape=jax.ShapeDtypeStruct((t, N_A * HEAD_DIM), MXU_DTYPE),
        grid=(N_A, nq),
        in_specs=[qspec, kv(N_A, 2), kv(N_A, 1), kv(N_A, 0), kv(2 * N_A, 2), kv(2 * N_A, 1), kv(2 * N_A, 0),
                  pl.BlockSpec((None, 1, 2 * A_KWIN), lambda h, i: (h, 0, 0))],
        out_specs=pl.BlockSpec((A_QBLK, HEAD_DIM), lambda h, i: (i, h)),
        scratch_shapes=[pltpu.VMEM((A_QBLK, A_KWIN), F32)],
        compiler_params=_params("parallel", "arbitrary"),
        name="chunk_attn_prompt",
    )(p_ab, p_ab, p_ab, p_ab, p_ab, p_ab, p_ab, frow)


def _chunk_attn_sample_kernel(q_ref, kn_ref, vn_ref, kc_ref, vc_ref, f_ref, o_ref):
    n = q_ref.shape[0]
    lc = kc_ref.shape[0]
    skew = _rel_bias_window(f_ref[...].reshape(1, -1), n)
    b = pltpu.roll(skew, 2 * A_KWIN - A_QBLK - (A_REACH - lc), axis=1)
    q = q_ref[...]
    s1 = _dot_nt(q, kc_ref[...]) * SCALE + b[:, :lc]
    s2 = _dot_nt(q, kn_ref[...]) * SCALE + b[:, lc:lc + n]
    m = jnp.maximum(jnp.max(s1, axis=-1, keepdims=True), jnp.max(s2, axis=-1, keepdims=True))
    e1 = jnp.exp(s1 - m)
    e2 = jnp.exp(s2 - m)
    l = jnp.sum(e1, axis=-1, keepdims=True) + jnp.sum(e2, axis=-1, keepdims=True)
    o = _dot(e1 / l, vc_ref[...]) + _dot(e2 / l, vn_ref[...])
    o_ref[...] = o.astype(o_ref.dtype)


def _chunk_attn_sample(p_ab, k_cache, v_cache, frow, nb):
    n = p_ab.shape[0] // nb
    lc = k_cache.shape[1]

    def new(off):
        return pl.BlockSpec((n, HEAD_DIM), lambda b, h: (b, off + h))

    cspec = pl.BlockSpec((None, lc, HEAD_DIM), lambda b, h: (b, 0, h))
    return pl.pallas_call(
        _chunk_attn_sample_kernel,
        out_shape=jax.ShapeDtypeStruct((nb * n, N_A * HEAD_DIM), MXU_DTYPE),
        grid=(nb, N_A),
        in_specs=[new(0), new(N_A), new(2 * N_A), cspec, cspec,
                  pl.BlockSpec((None, 1, 2 * A_KWIN), lambda b, h: (h, 0, 0))],
        out_specs=pl.BlockSpec((n, HEAD_DIM), lambda b, h: (b, h)),
        compiler_params=_params("parallel", "parallel"),
        name="chunk_attn_sample",
    )(p_ab, p_ab, p_ab, k_cache, v_cache, frow)


def _logf_kernel(p_ref, b_ref, o_ref):
    x = p_ref[...] + b_ref[...]
    o_ref[...] = -_softplus(-x)


def _log_forget(p_f, bias_row):
    m = p_f.shape[0]
    tm = _row_tile(m, 2048)
    return pl.pallas_call(
        _logf_kernel,
        out_shape=jax.ShapeDtypeStruct((m, LANE), F32),
        grid=(m // tm,),
        in_specs=[pl.BlockSpec((tm, LANE), lambda i: (i, 0)), pl.BlockSpec((1, LANE), lambda i: (0, 0))],
        out_specs=pl.BlockSpec((tm, LANE), lambda i: (i, 0)),
        compiler_params=_params("parallel"),
        name="log_forget",
    )(p_f, bias_row)


def _cumsum_kernel(x_ref, o_ref, carry_ref):
    @pl.when(pl.program_id(1) == 0)
    def _():
        carry_ref[...] = jnp.zeros_like(carry_ref)

    tl = x_ref.shape[-1]
    r = lax.broadcasted_iota(jnp.int32, (tl, tl), 0)
    c = lax.broadcasted_iota(jnp.int32, (tl, tl), 1)
    tri = jnp.where(r <= c, 1.0, 0.0).astype(BF16)
    cs = _dot_split(x_ref[...], tri) + carry_ref[:, :1]
    o_ref[...] = cs
    carry_ref[...] = jnp.broadcast_to(cs[:, tl - 1:tl], carry_ref.shape)


def _cumsum_lanes(x):
    g, r, l = x.shape
    tl = 512
    lp = -(-l // tl) * tl
    xp = jnp.pad(x, ((0, 0), (0, 0), (0, lp - l)))
    out = pl.pallas_call(
        _cumsum_kernel,
        out_shape=jax.ShapeDtypeStruct((g, r, lp), F32),
        grid=(g, lp // tl),
        in_specs=[pl.BlockSpec((None, r, tl), lambda b, j: (b, 0, j))],
        out_specs=pl.BlockSpec((None, r, tl), lambda b, j: (b, 0, j)),
        scratch_shapes=[pltpu.VMEM((r, LANE), F32)],
        compiler_params=_params("parallel", "arbitrary"),
        name="cumsum",
    )(xp)
    return out[:, :, :l]


def _fox_prompt_kernel(q_ref, k_ref, v_ref, cq_ref, ck_ref, o_ref, m_ref, l_ref, acc_ref):
    i = pl.program_id(1)
    j = pl.program_id(2)
    tq = q_ref.shape[0]
    tk = k_ref.shape[0]

    @pl.when(j == 0)
    def _():
        m_ref[...] = jnp.full_like(m_ref, -jnp.inf)
        l_ref[...] = jnp.zeros_like(l_ref)
        acc_ref[...] = jnp.zeros_like(acc_ref)

    @pl.when(j <= i)
    def _():
        s = _dot_nt(q_ref[...], k_ref[...]) * SCALE
        s = s + cq_ref[...] - ck_ref[...]
        qpos = i * tq + lax.broadcasted_iota(jnp.int32, s.shape, 0)
        kpos = j * tk + lax.broadcasted_iota(jnp.int32, s.shape, 1)
        s = jnp.where(kpos <= qpos, s, -jnp.inf)
        m_prev = m_ref[...]
        m_new = jnp.maximum(m_prev, jnp.max(s, axis=-1, keepdims=True))
        alpha = jnp.exp(m_prev - m_new)
        e = jnp.exp(s - m_new)
        l_ref[...] = alpha * l_ref[...] + jnp.sum(e, axis=-1, keepdims=True)
        acc_ref[...] = alpha * acc_ref[...] + _dot(e, v_ref[...])
        m_ref[...] = m_new

    @pl.when(j == i)
    def _():
        o_ref[...] = (acc_ref[...] / l_ref[...]).astype(o_ref.dtype)


def _fox_prompt(p_ab, c):
    t = p_ab.shape[0]
    tq = _row_tile(t, 512)
    nq = t // tq
    off = 3 * N_A

    def kv(o):
        return pl.BlockSpec((tq, HEAD_DIM), lambda h, i, j: (jnp.minimum(j, i), o + h))

    return pl.pallas_call(
        _fox_prompt_kernel,
        out_shape=jax.ShapeDtypeStruct((t, N_B * HEAD_DIM), MXU_DTYPE),
        grid=(N_B, nq, nq),
        in_specs=[pl.BlockSpec((tq, HEAD_DIM), lambda h, i, j: (i, off + h)),
                  kv(off + N_B), kv(off + 2 * N_B),
                  pl.BlockSpec((None, tq, 1), lambda h, i, j: (h, i, 0)),
                  pl.BlockSpec((None, 1, tq), lambda h, i, j: (h, 0, jnp.minimum(j, i)))],
        out_specs=pl.BlockSpec((tq, HEAD_DIM), lambda h, i, j: (i, h)),
        scratch_shapes=[pltpu.VMEM((tq, 1), F32), pltpu.VMEM((tq, 1), F32), pltpu.VMEM((tq, HEAD_DIM), F32)],
        compiler_params=_params("parallel", "parallel", "arbitrary"),
        name="fox_prompt",
    )(p_ab, p_ab, p_ab, c.reshape(N_B, t, 1), c.reshape(N_B, 1, t))


def _fox_sample_kernel(q_ref, kn_ref, vn_ref, kc_ref, vc_ref, cq_ref, ck_ref, o_ref):
    n = q_ref.shape[0]
    pl_ = kc_ref.shape[0]
    q = q_ref[...]
    cq = cq_ref[...]
    ck = ck_ref[...]
    s1 = _dot_nt(q, kc_ref[...]) * SCALE + cq - ck[:, :pl_]
    s2 = _dot_nt(q, kn_ref[...]) * SCALE + cq - ck[:, pl_:pl_ + n]
    r = lax.broadcasted_iota(jnp.int32, (n, n), 0)
    c = lax.broadcasted_iota(jnp.int32, (n, n), 1)
    s2 = jnp.where(c <= r, s2, -jnp.inf)
    m = jnp.maximum(jnp.max(s1, axis=-1, keepdims=True), jnp.max(s2, axis=-1, keepdims=True))
    e1 = jnp.exp(s1 - m)
    e2 = jnp.exp(s2 - m)
    l = jnp.sum(e1, axis=-1, keepdims=True) + jnp.sum(e2, axis=-1, keepdims=True)
    o = _dot(e1 / l, vc_ref[...]) + _dot(e2 / l, vn_ref[...])
    o_ref[...] = o.astype(o_ref.dtype)


def _fox_sample(p_ab, k_cache, v_cache, c, nb):
    n = p_ab.shape[0] // nb
    past = k_cache.shape[1]
    off = 3 * N_A
    cpad = -(-(past + n) // LANE) * LANE
    ck = jnp.pad(c, ((0, 0), (0, 0), (0, cpad - past - n))).reshape(nb, N_B, 1, cpad)
    cq = c[:, :, past:].reshape(nb, N_B, n, 1)

    def new(o):
        return pl.BlockSpec((n, HEAD_DIM), lambda b, h: (b, o + h))

    cspec = pl.BlockSpec((None, past, HEAD_DIM), lambda b, h: (b, 0, h))
    return pl.pallas_call(
        _fox_sample_kernel,
        out_shape=jax.ShapeDtypeStruct((nb * n, N_B * HEAD_DIM), MXU_DTYPE),
        grid=(nb, N_B),
        in_specs=[new(off), new(off + N_B), new(off + 2 * N_B), cspec, cspec,
                  pl.BlockSpec((None, None, n, 1), lambda b, h: (b, h, 0, 0)),
                  pl.BlockSpec((None, None, 1, cpad), lambda b, h: (b, h, 0, 0))],
        out_specs=pl.BlockSpec((n, HEAD_DIM), lambda b, h: (b, h)),
        compiler_params=_params("parallel", "parallel"),
        name="fox_sample",
    )(p_ab, p_ab, p_ab, k_cache, v_cache, cq, ck)


C_R, C_K, C_V = 0, W_C, 2 * W_C
C_ZW = 3 * W_C
C_ZA = C_ZW + LORA_PAD
C_ZG = C_ZA + LORA_PAD
LORA_G = 256
C_COLS = C_ZG + LORA_G


def _rwkv_prep_kernel(pc_ref, sh_ref, mu_ref, w0_ref, w2_ref, a0_ref, a2_ref, g2_ref, kk_ref, ka_ref, rk_ref,
                      seg_ref, r_out, w_out, k_out, v_out, a_out, b_out, gate_out, bonus_out, sh_out, carry_ref):
    i = pl.program_id(1)

    @pl.when(i == 0)
    def _():
        carry_ref[...] = jnp.broadcast_to(sh_ref[...], carry_ref.shape)

    pc = pc_ref[...]
    tm = pc.shape[0]
    row = lax.broadcasted_iota(jnp.int32, pc.shape, 0)
    prev = jnp.where(row == 0, carry_ref[:1, :], pltpu.roll(pc, 1, axis=0))
    carry_ref[...] = jnp.broadcast_to(pc[tm - 1:tm, :], carry_ref.shape)
    sh_out[...] = pc[tm - 1:tm, :]
    z = pc + (prev - pc) * mu_ref[...]
    zr = z[:, C_R:C_R + W_C]
    zk = z[:, C_K:C_K + W_C]
    zv = z[:, C_V:C_V + W_C]
    zw = z[:, C_ZW:C_ZW + LORA_PAD]
    za = z[:, C_ZA:C_ZA + LORA_PAD]
    zg = z[:, C_ZG:C_ZG + LORA_G]
    w_raw = -_softplus(-(w0_ref[...] + _dot(jnp.tanh(zw), w2_ref[...]))) - 0.5
    decay = jnp.exp(-jnp.exp(w_raw))
    a = _sigmoid(a0_ref[...] + _dot(za, a2_ref[...]))
    gate = _dot(_sigmoid(zg), g2_ref[...])
    seg = seg_ref[...]
    kk = zk * kk_ref[...]
    kk = kk * lax.rsqrt(_dot_split(kk * kk, seg) + 1e-12)
    kmod = zk * (1.0 + (a - 1.0) * ka_ref[...])
    r_out[...] = zr
    w_out[...] = decay
    k_out[...] = kmod
    v_out[...] = zv
    a_out[...] = -kk
    b_out[...] = kk * a
    gate_out[...] = gate
    bonus_out[...] = _dot_split(zr * kmod * rk_ref[...], seg) * zv


def _rwkv_prep(p_c, shift, lw, nb):
    m = p_c.shape[0]
    t = m // nb
    tm = _row_tile(t, 256)
    nt = t // tm
    row = pl.BlockSpec((tm, W_C), lambda b, i: (b * nt + i, 0))

    def const(shape):
        return pl.BlockSpec(shape, lambda b, i: (0,) * len(shape))

    outs = pl.pallas_call(
        _rwkv_prep_kernel,
        out_shape=[jax.ShapeDtypeStruct((m, W_C), F32)] * 8 + [jax.ShapeDtypeStruct((nb, 1, C_COLS), F32)],
        grid=(nb, nt),
        in_specs=[pl.BlockSpec((tm, C_COLS), lambda b, i: (b * nt + i, 0)),
                  pl.BlockSpec((None, 1, C_COLS), lambda b, i: (b, 0, 0)),
                  const((1, C_COLS)), const((1, W_C)), const((LORA_PAD, W_C)), const((1, W_C)),
                  const((LORA_PAD, W_C)), const((LORA_G, W_C)), const((1, W_C)), const((1, W_C)),
                  const((1, W_C)), const((W_C, W_C))],
        out_specs=[row] * 8 + [pl.BlockSpec((None, 1, C_COLS), lambda b, i: (b, 0, 0))],
        scratch_shapes=[pltpu.VMEM((8, C_COLS), F32)],
        compiler_params=_params("parallel", "arbitrary"),
        name="rwkv_prep",
    )(p_c, shift, lw["mu"], lw["w0"], lw["w2"], lw["a0"], lw["a2"], lw["g2"], lw["k_k"], lw["k_a"], lw["r_k"],
      lw["seg"])
    return outs


def _wkv_scan_kernel(r_ref, w_ref, k_ref, a_ref, b_ref, vt_ref, s0_ref, yt_ref, s_out, s_ref):
    j = pl.program_id(1)
    tb = vt_ref.shape[1]

    @pl.when(j == 0)
    def _():
        s_ref[...] = s0_ref[...]

    yt_ref[...] = jnp.zeros_like(yt_ref)
    lane = lax.broadcasted_iota(jnp.int32, (C_HEAD, tb), 1)

    def step(t, carry):
        hit = lane == t
        for h in range(N_C):
            rows = slice(h * C_HEAD, (h + 1) * C_HEAD)
            s = s_ref[h]
            a = a_ref[h, pl.ds(t, 1), :]
            w = w_ref[h, pl.ds(t, 1), :]
            b = b_ref[h, pl.ds(t, 1), :]
            k = k_ref[h, pl.ds(t, 1), :]
            r = r_ref[h, pl.ds(t, 1), :]
            v = jnp.sum(jnp.where(hit, vt_ref[rows, :], 0.0), axis=-1, keepdims=True)
            sa = jnp.sum(s * a, axis=-1, keepdims=True)
            s = s * w + sa * b + v * k
            s_ref[h] = s
            y = jnp.sum(s * r, axis=-1, keepdims=True)
            yt_ref[rows, :] = jnp.where(hit, y, yt_ref[rows, :])
        return carry

    lax.fori_loop(0, tb, step, 0)

    @pl.when(j == pl.num_programs(1) - 1)
    def _():
        s_out[...] = s_ref[...]


def _wkv_scan(r, w, k, a, b, v, s0, nb):
    m = r.shape[0]
    t = m // nb
    tb = _row_tile(t, LANE)
    nt = t // tb

    def heads(x):
        return x.reshape(m, N_C, C_HEAD).transpose(1, 0, 2)

    vt = v.reshape(nb, t, W_C).transpose(0, 2, 1)
    hspec = pl.BlockSpec((N_C, tb, C_HEAD), lambda g, j: (0, g * nt + j, 0))
    tspec = pl.BlockSpec((None, W_C, tb), lambda g, j: (g, 0, j))
    sspec = pl.BlockSpec((None, N_C, C_HEAD, C_HEAD), lambda g, j: (g, 0, 0, 0))
    yt, s = pl.pallas_call(
        _wkv_scan_kernel,
        out_shape=[jax.ShapeDtypeStruct((nb, W_C, t), F32), jax.ShapeDtypeStruct(s0.shape, F32)],
        grid=(nb, nt),
        in_specs=[hspec] * 5 + [tspec, sspec],
        out_specs=[tspec, sspec],
        scratch_shapes=[pltpu.VMEM((N_C, C_HEAD, C_HEAD), F32)],
        compiler_params=_params("parallel", "arbitrary"),
        name="wkv_scan",
    )(heads(r), heads(w), heads(k), heads(a), heads(b), vt, s0)
    return yt.transpose(0, 2, 1).reshape(m, W_C), s


def _rwkv_post_kernel(y_ref, bonus_ref, gate_ref, lw_ref, lb_ref, seg_ref, o_ref):
    y = y_ref[...]
    seg = seg_ref[...]
    mu = _dot_split(y, seg) * (1.0 / C_HEAD)
    d = y - mu
    var = _dot_split(d * d, seg) * (1.0 / C_HEAD)
    yn = d * lax.rsqrt(var + LNX_EPS) * lw_ref[...] + lb_ref[...]
    o_ref[...] = ((yn + bonus_ref[...]) * gate_ref[...]).astype(o_ref.dtype)


def _rwkv_post(y, bonus, gate, lw):
    m = y.shape[0]
    tm = _row_tile(m, 512)
    row = pl.BlockSpec((tm, W_C), lambda i: (i, 0))
    vec = pl.BlockSpec((1, W_C), lambda i: (0, 0))
    return pl.pallas_call(
        _rwkv_post_kernel,
        out_shape=jax.ShapeDtypeStruct((m, W_C), MXU_DTYPE),
        grid=(m // tm,),
        in_specs=[row, row, row, vec, vec, pl.BlockSpec((W_C, W_C), lambda i: (0, 0))],
        out_specs=row,
        compiler_params=_params("parallel"),
        name="rwkv_post",
    )(y, bonus, gate, lw["lnx_w"], lw["lnx_b"], lw["seg"])


def _branch_mix_kernel(oa_ref, ob_ref, oc_ref, ga_ref, gb_ref, gc_ref, wa_ref, wb_ref, wc_ref, o_ref):
    d = functools.partial(jnp.dot, preferred_element_type=F32)
    mix = (_sigmoid(ga_ref[...]) * d(oa_ref[...], wa_ref[...])
           + _sigmoid(gb_ref[...]) * d(ob_ref[...], wb_ref[...])
           + _sigmoid(gc_ref[...]) * d(oc_ref[...], wc_ref[...]))
    o_ref[...] = mix.astype(o_ref.dtype)


def _branch_mix(oa, ob, oc, p_g, wa, wb, wc):
    m = oa.shape[0]
    d = wa.shape[1]
    tm = _row_tile(m, 512)

    def row(width):
        return pl.BlockSpec((tm, width), lambda i: (i, 0))

    def gate(j):
        return pl.BlockSpec((tm, d), lambda i: (i, j))

    def full(w):
        return pl.BlockSpec(w.shape, lambda i: (0, 0))

    return pl.pallas_call(
        _branch_mix_kernel,
        out_shape=jax.ShapeDtypeStruct((m, d), MXU_DTYPE),
        grid=(m // tm,),
        in_specs=[row(oa.shape[1]), row(ob.shape[1]), row(oc.shape[1]), gate(0), gate(1), gate(2),
                  full(wa), full(wb), full(wc)],
        out_specs=row(d),
        compiler_params=_params("parallel"),
        name="branch_mix",
    )(oa, ob, oc, p_g, p_g, p_g, wa, wb, wc)


def _cross_kernel(x_ref, gi_ref, go_ref, wq_ref, wo_ref, mk_ref, mv_ref, o_ref):
    x = x_ref[...]
    q = _dot(_rms(x, gi_ref[...]), wq_ref[...])
    mk = mk_ref[...]
    mv = mv_ref[...]
    heads = []
    for h in range(N_X):
        cols = slice(h * HEAD_DIM, (h + 1) * HEAD_DIM)
        s = _dot_nt(q[:, cols], mk[:, cols]) * SCALE
        e = jnp.exp(s - jnp.max(s, axis=-1, keepdims=True))
        p = e / jnp.sum(e, axis=-1, keepdims=True)
        heads.append(_dot(p, mv[:, cols]))
    o = jnp.concatenate(heads, axis=-1)
    o_ref[...] = x + _rms(_dot(o, wo_ref[...]), go_ref[...])


def _cross_attn(x, g_in, g_out, wq, wo, mk, mv, nb):
    m, d = x.shape
    t = m // nb
    tm = _row_tile(t, 512)
    nt = t // tm
    row = pl.BlockSpec((tm, d), lambda b, i: (b * nt + i, 0))
    vec = pl.BlockSpec((1, d), lambda b, i: (0, 0))
    mem = pl.BlockSpec((None,) + mk.shape[1:], lambda b, i: (b, 0, 0))
    return pl.pallas_call(
        _cross_kernel,
        out_shape=jax.ShapeDtypeStruct((m, d), F32),
        grid=(nb, nt),
        in_specs=[row, vec, vec, pl.BlockSpec(wq.shape, lambda b, i: (0, 0)),
                  pl.BlockSpec(wo.shape, lambda b, i: (0, 0)), mem, mem],
        out_specs=row,
        compiler_params=_params("parallel", "parallel"),
        name="cross_attn",
    )(x, g_in.reshape(1, d), g_out.reshape(1, d), wq, wo, mk, mv)


def _ffn_kernel(x_ref, gi_ref, go_ref, wg_ref, wu_ref, wo_ref, o_ref, h_ref, acc_ref):
    j = pl.program_id(1)

    @pl.when(j == 0)
    def _():
        h_ref[...] = _rms(x_ref[...], gi_ref[...]).astype(h_ref.dtype)
        acc_ref[...] = jnp.zeros_like(acc_ref)

    h = h_ref[...]
    gte = jnp.dot(h, wg_ref[...], preferred_element_type=F32)
    up = jnp.dot(h, wu_ref[...], preferred_element_type=F32)
    act = gte * _sigmoid(gte) * up
    acc_ref[...] += _dot(act, wo_ref[...])

    @pl.when(j == pl.num_programs(1) - 1)
    def _():
        o_ref[...] = x_ref[...] + _rms(acc_ref[...], go_ref[...])


def _ffn(x, g_in, g_out, w_in, w_out):
    m, d = x.shape
    dff = w_out.shape[0]
    tm = _row_tile(m, 512)
    tf = 512
    assert dff % tf == 0
    nf = dff // tf
    row = pl.BlockSpec((tm, d), lambda i, j: (i, 0))
    vec = pl.BlockSpec((1, d), lambda i, j: (0, 0))
    return pl.pallas_call(
        _ffn_kernel,
        out_shape=jax.ShapeDtypeStruct((m, d), F32),
        grid=(m // tm, nf),
        in_specs=[row, vec, vec,
                  pl.BlockSpec((d, tf), lambda i, j: (0, j)),
                  pl.BlockSpec((d, tf), lambda i, j: (0, nf + j)),
                  pl.BlockSpec((tf, d), lambda i, j: (j, 0))],
        out_specs=row,
        scratch_shapes=[pltpu.VMEM((tm, d), MXU_DTYPE), pltpu.VMEM((tm, d), F32)],
        compiler_params=_params("parallel", "arbitrary"),
        name="ffn",
    )(x, g_in.reshape(1, d), g_out.reshape(1, d), w_in, w_in, w_out)


def _pad_cols(w, width):
    return jnp.pad(w, ((0, 0), (0, width - w.shape[1])))


def _c_layout(x):
    lw = (C_ZA - C_ZW)
    n_lora = (x.shape[-1] - 3 * W_C - LORA_G) // 2
    pad = [(0, 0)] * (x.ndim - 1)
    zw = jnp.pad(x[..., 3 * W_C:3 * W_C + n_lora], pad + [(0, lw - n_lora)])
    za = jnp.pad(x[..., 3 * W_C + n_lora:3 * W_C + 2 * n_lora], pad + [(0, lw - n_lora)])
    return jnp.concatenate([x[..., :3 * W_C], zw, za, x[..., 3 * W_C + 2 * n_lora:]], axis=-1)


def _c_unlayout(x, n_lora):
    return jnp.concatenate([x[..., :3 * W_C], x[..., C_ZW:C_ZW + n_lora], x[..., C_ZA:C_ZA + n_lora],
                            x[..., C_ZG:]], axis=-1)


def _layer_weights(l, norm_g, mem_norm_g, w_in, a_rel_bias, b_forget_bias, c_mu, c_w0, c_w2, c_a0, c_a2, c_g2,
                   c_k_k, c_k_a, c_r_k, c_lnx_w, c_lnx_b, w_branch_a, w_branch_b, w_branch_c, w_out, w_xq,
                   w_mkv, w_xo, w_ffn_in, w_ffn_out):
    wa_cols = 3 * N_A * HEAD_DIM
    wb_cols = 3 * N_B * HEAD_DIM
    off_f = wa_cols + wb_cols
    off_c = off_f + N_B
    c_proj = c_mu.shape[1]
    off_g = off_c + c_proj
    n_lora = c_w2.shape[1]
    wi = w_in[l]
    cast = lambda w: w.astype(MXU_DTYPE)
    row = lambda v: v.reshape(1, -1).astype(F32)
    pad_rows = lambda w: jnp.pad(w, ((0, LORA_PAD - w.shape[0]), (0, 0)))
    seg_id = jnp.arange(W_C) // C_HEAD
    return dict(
        g=norm_g[l], mem_g=mem_norm_g[l],
        w_ab=cast(wi[:, :off_f]),
        w_f=cast(_pad_cols(wi[:, off_f:off_c], LANE)),
        w_c=cast(_c_layout(wi[:, off_c:off_g])),
        w_g=cast(wi[:, off_g:]),
        frow=_rel_bias_rows(a_rel_bias[l]),
        f_bias=_pad_cols(row(b_forget_bias[l]), LANE),
        c=dict(mu=_c_layout(row(c_mu[l])), w0=row(c_w0[l]), w2=cast(pad_rows(c_w2[l])), a0=row(c_a0[l]),
               a2=cast(pad_rows(c_a2[l])), g2=cast(c_g2[l]), k_k=row(c_k_k[l]), k_a=row(c_k_a[l]),
               r_k=row(c_r_k[l]), lnx_w=row(c_lnx_w[l]), lnx_b=row(c_lnx_b[l]),
               seg=(seg_id[:, None] == seg_id[None, :]).astype(BF16)),
        n_lora=n_lora,
        w_ba=cast(w_branch_a[l]), w_bb=cast(w_branch_b[l]), w_bc=cast(w_branch_c[l]), w_out=cast(w_out[l]),
        w_xq=cast(w_xq[l]), w_mkv=cast(w_mkv[l]), w_xo=cast(w_xo[l]),
        w_ffn_in=cast(w_ffn_in[l]), w_ffn_out=cast(w_ffn_out[l]),
    )


def _layer(x, lw, nb, mem=None, cache=None):
    m, d = x.shape
    t = m // nb
    g = lw["g"]
    p_ab = _rms_matmul(x, g[0], lw["w_ab"], 768)
    p_f = _rms_matmul(x, g[0], lw["w_f"], LANE)
    p_c = _rms_matmul(x, g[0], lw["w_c"], C_COLS // 2)
    p_g = _rms_matmul(x, g[0], lw["w_g"], 768)
    logf = _log_forget(p_f, lw["f_bias"])[:, :N_B]
    wa = N_A * HEAD_DIM
    wb = N_B * HEAD_DIM
    if cache is None:
        oa = _chunk_attn_prompt(p_ab, lw["frow"])
        c = _cumsum_lanes(logf.T.reshape(1, N_B, t))[0]
        ob = _fox_prompt(p_ab, c)
        shift0 = jnp.zeros((nb, 1, C_COLS), F32)
        wkv0 = jnp.zeros((nb, N_C, C_HEAD, C_HEAD), F32)
        pm = _rms_matmul(mem.reshape(-1, d), lw["mem_g"], lw["w_mkv"], 512)
        n_mem = mem.shape[1]
        mk = pm[:, :N_X * HEAD_DIM].reshape(nb, n_mem, N_X * HEAD_DIM)
        mv = pm[:, N_X * HEAD_DIM:].reshape(nb, n_mem, N_X * HEAD_DIM)
    else:
        ak, av, bk, bv, blf, shift0, wkv0, mk, mv = cache
        past = bk.shape[1]
        oa = _chunk_attn_sample(p_ab, ak.reshape(nb, ak.shape[1], wa), av.reshape(nb, av.shape[1], wa),
                                lw["frow"], nb)
        lf_all = jnp.concatenate([blf.astype(F32), logf.reshape(nb, t, N_B)], axis=1)
        c = _cumsum_lanes(lf_all.transpose(0, 2, 1))
        ob = _fox_sample(p_ab, bk.reshape(nb, past, wb), bv.reshape(nb, past, wb), c, nb)
        shift0 = _c_layout(shift0.astype(F32))
        wkv0 = wkv0.astype(F32)
        mk = mk.reshape(nb, mk.shape[1], N_X * HEAD_DIM)
        mv = mv.reshape(nb, mv.shape[1], N_X * HEAD_DIM)
    r, w, k, v, a, b, gate, bonus, shift_new = _rwkv_prep(p_c, shift0, lw["c"], nb)
    y, wkv_new = _wkv_scan(r, w, k, a, b, v, wkv0, nb)
    oc = _rwkv_post(y, bonus, gate, lw["c"])
    mix = _branch_mix(oa, ob, oc, p_g, lw["w_ba"], lw["w_bb"], lw["w_bc"])
    x = _matmul_norm_res(mix, lw["w_out"], g[1], x)
    x = _cross_attn(x, g[2], g[3], lw["w_xq"], lw["w_xo"], mk, mv, nb)
    x = _ffn(x, g[4], g[5], lw["w_ffn_in"], lw["w_ffn_out"])
    n_keep = min(A_REACH, t)
    p3 = p_ab.reshape(nb, t, -1)
    ka = p3[:, t - n_keep:, wa:2 * wa].reshape(nb, n_keep, N_A, HEAD_DIM)
    va = p3[:, t - n_keep:, 2 * wa:3 * wa].reshape(nb, n_keep, N_A, HEAD_DIM)
    kf = p3[:, :, 3 * wa + wb:3 * wa + 2 * wb].reshape(nb, t, N_B, HEAD_DIM)
    vf = p3[:, :, 3 * wa + 2 * wb:3 * wa + 3 * wb].reshape(nb, t, N_B, HEAD_DIM)
    rows = (ka, va, kf, vf, logf.reshape(nb, t, N_B), _c_unlayout(shift_new, lw["n_lora"]), wkv_new)
    mems = (mk.reshape(nb, -1, N_X, HEAD_DIM), mv.reshape(nb, -1, N_X, HEAD_DIM))
    return x, rows, mems


def kernel(x_prompt, x_sample, mem_prompt, cache_a_k, cache_a_v, cache_b_k, cache_b_v, cache_b_logf, state_c_shift, state_c_wkv, cache_m_k, cache_m_v, norm_g, mem_norm_g, w_in, a_rel_bias, b_forget_bias, c_mu, c_w0, c_w2, c_a0, c_a2, c_g2, c_k_k, c_k_a, c_r_k, c_lnx_w, c_lnx_b, w_branch_a, w_branch_b, w_branch_c, w_out, w_xq, w_mkv, w_xo, w_ffn_in, w_ffn_out):
    depth = w_in.shape[0]
    bp, tp, d = x_prompt.shape
    bs, ts, _ = x_sample.shape
    y_p = x_prompt.reshape(bp * tp, d)
    y_s = x_sample.reshape(bs * ts, d)
    rows_p, mems_p, rows_s = [], [], []
    for l in range(depth):
        lw = _layer_weights(l, norm_g, mem_norm_g, w_in, a_rel_bias, b_forget_bias, c_mu, c_w0, c_w2, c_a0, c_a2,
                            c_g2, c_k_k, c_k_a, c_r_k, c_lnx_w, c_lnx_b, w_branch_a, w_branch_b, w_branch_c,
                            w_out, w_xq, w_mkv, w_xo, w_ffn_in, w_ffn_out)
        y_p, r_p, m_p = _layer(y_p, lw, bp, mem=mem_prompt)
        rows_p.append(r_p)
        mems_p.append(m_p)
        cache = (cache_a_k[l], cache_a_v[l], cache_b_k[l], cache_b_v[l], cache_b_logf[l],
                 state_c_shift[l], state_c_wkv[l], cache_m_k[l], cache_m_v[l])
        y_s, r_s, _ = _layer(y_s, lw, bs, cache=cache)
        rows_s.append(r_s)
    a_k_p, a_v_p, b_k_p, b_v_p, b_lf_p, c_sh_p, c_wkv_p = [jnp.stack(t) for t in zip(*rows_p)]
    m_k_p, m_v_p = [jnp.stack(t) for t in zip(*mems_p)]
    a_k_s, a_v_s, b_k_s, b_v_s, b_lf_s, c_sh_s, c_wkv_s = [jnp.stack(t) for t in zip(*rows_s)]
    return (y_p.reshape(bp, tp, d), y_s.reshape(bs, ts, d), a_k_p, a_v_p, b_k_p, b_v_p, b_lf_p, c_sh_p, c_wkv_p,
            m_k_p, m_v_p, a_k_s, a_v_s, b_k_s, b_v_s, b_lf_s, c_sh_s, c_wkv_s)
```
